```python
import jax, jax.numpy as jnp
from jax import lax
import numpy as np

D_MODEL = 1024
BATCH = 8
SEQ = 4096
DEPTH = 4

GRID_W = 64
CTX_LEN = 256
EPS = 1e-6
EXPAND = 2
D_INNER = EXPAND * D_MODEL
D_CONV = D_INNER // 2
HEAD_DIM = 64
N_Q_HEADS = (D_INNER - D_CONV) // HEAD_DIM
N_KV_HEADS = 4
GQA_GROUP = N_Q_HEADS // N_KV_HEADS
CONF_WIDTH = 31
WINDOW = 128
Q_BLOCK = 128
BAND = Q_BLOCK + 2 * WINDOW
ROPE_THETA = 10000.0
NEG_INF = -1e30
EV_A = 0
EV_Q = 2 * D_CONV
EV_K = EV_Q + N_Q_HEADS * HEAD_DIM
EV_V = EV_K + N_KV_HEADS * HEAD_DIM
EV_G = EV_V + N_KV_HEADS * HEAD_DIM
EV_COLS = EV_G + D_INNER
SC_WIDTH = 3
OD_COLS = 4 * D_INNER
N_EVEN = (DEPTH + 1) // 2
N_ODD = DEPTH // 2

kernel_name = "hybrid_conformer_swa_shortconv_dit"


def rms_norm(x, g):
    xf = x.astype(jnp.float32)
    y = xf * lax.rsqrt(jnp.mean(xf * xf, axis=-1, keepdims=True) + EPS)
    return (y * g.astype(jnp.float32)).astype(x.dtype)


def layer_norm(x, g, b):
    xf = x.astype(jnp.float32)
    mu = jnp.mean(xf, axis=-1, keepdims=True)
    var = jnp.mean(jnp.square(xf - mu), axis=-1, keepdims=True)
    y = (xf - mu) * lax.rsqrt(var + EPS)
    return (y * g.astype(jnp.float32) + b.astype(jnp.float32)).astype(x.dtype)


def adaln(cond, w, b):
    m = jax.nn.silu(cond) @ w + b
    return jnp.split(m, 3, axis=-1)


def modulate(xn, shift, scale):
    return xn * (1 + scale) + shift


def depthwise_conv(u, w, b):
    k = w.shape[0]
    out = lax.conv_general_dilated(
        u, w[:, None, :].astype(u.dtype), window_strides=(1,),
        padding=[(k // 2, k // 2)], dimension_numbers=("NWC", "WIO", "NWC"),
        feature_group_count=u.shape[-1])
    return out + b.astype(u.dtype)


def axial_rope(t, row, col):
    half = t.shape[-1] // 2
    quarter = half // 2
    freqs = ROPE_THETA ** (-jnp.arange(quarter, dtype=jnp.float32) / quarter)

    def rot(u, pos):
        ang = pos[:, None] * freqs[None, :]
        cos = jnp.cos(ang)[:, None, :].astype(u.dtype)
        sin = jnp.sin(ang)[:, None, :].astype(u.dtype)
        u1, u2 = u[..., :quarter], u[..., quarter:]
        return jnp.concatenate([u1 * cos - u2 * sin, u1 * sin + u2 * cos], axis=-1)

    return jnp.concatenate([rot(t[..., :half], row), rot(t[..., half:], col)], axis=-1)


def split_heads(t, n_heads):
    return t.reshape(t.shape[:-1] + (n_heads, HEAD_DIM))


def conformer_conv(a_val, a_gate, dw_w, dw_b, ln_g, ln_b):
    u = a_val * jax.nn.sigmoid(a_gate)
    u = depthwise_conv(u, dw_w, dw_b)
    return jax.nn.silu(layer_norm(u, ln_g, ln_b))


def context_attention(qc, kc, vc, sink):
    bsz, l = qc.shape[:2]
    scale = HEAD_DIM ** -0.5
    s = jnp.einsum("bqhgd,bkhd->bhgqk", qc, kc).astype(jnp.float32) * scale
    sk = jnp.broadcast_to(sink.reshape(N_KV_HEADS, GQA_GROUP)[None, :, :, None, None].astype(jnp.float32),
                          s.shape[:-1] + (1,))
    p = jax.nn.softmax(jnp.concatenate([s, sk], axis=-1), axis=-1)[..., :-1]
    o = jnp.einsum("bhgqk,bkhd->bqhgd", p.astype(vc.dtype), vc)
    return o.reshape(bsz, l, N_Q_HEADS * HEAD_DIM)


def latent_window_attention(q, k, v, kc, vc, sink):
    bsz, s_len = q.shape[:2]
    n_blk = s_len // Q_BLOCK
    n_ctx = kc.shape[1]
    scale = HEAD_DIM ** -0.5
    pad = ((0, 0), (WINDOW, WINDOW), (0, 0), (0, 0))
    kp = jnp.pad(k, pad)
    vp = jnp.pad(v, pad)
    qb = q.reshape(bsz, n_blk, Q_BLOCK, N_KV_HEADS, GQA_GROUP, HEAD_DIM).transpose(1, 0, 2, 3, 4, 5)
    qi = jnp.arange(Q_BLOCK)[:, None]
    kj = jnp.arange(BAND)[None, :]
    rel = kj - qi
    in_window = (rel >= 0) & (rel <= 2 * WINDOW)
    sink_l = sink.reshape(N_KV_HEADS, GQA_GROUP)[None, :, :, None, None].astype(jnp.float32)

    def one_block(args):
        blk, qblk = args
        start = blk * Q_BLOCK
        kb = lax.dynamic_slice_in_dim(kp, start, BAND, axis=1)
        vb = lax.dynamic_slice_in_dim(vp, start, BAND, axis=1)
        j = start + kj - WINDOW
        valid = in_window & (j >= 0) & (j < s_len)
        s_loc = jnp.einsum("bqhgd,bkhd->bhgqk", qblk, kb).astype(jnp.float32) * scale
        s_loc = jnp.where(valid, s_loc, NEG_INF)
        s_ctx = jnp.einsum("bqhgd,bkhd->bhgqk", qblk, kc).astype(jnp.float32) * scale
        sk = jnp.broadcast_to(sink_l, s_ctx.shape[:-1] + (1,))
        p = jax.nn.softmax(jnp.concatenate([s_loc, s_ctx, sk], axis=-1), axis=-1)
        p_loc = p[..., :BAND].astype(v.dtype)
        p_ctx = p[..., BAND:BAND + n_ctx].astype(v.dtype)
        return (jnp.einsum("bhgqk,bkhd->bqhgd", p_loc, vb)
                + jnp.einsum("bhgqk,bkhd->bqhgd", p_ctx, vc))

    o = lax.map(one_block, (jnp.arange(n_blk), qb))
    return o.transpose(1, 0, 2, 3, 4, 5).reshape(bsz, s_len, N_Q_HEADS * HEAD_DIM)


def short_gated_conv(h, conv_w, conv_b, w_out):
    bg = h[..., :D_INNER]
    cg = h[..., D_INNER:2 * D_INNER]
    u = h[..., 2 * D_INNER:3 * D_INNER]
    z = h[..., 3 * D_INNER:]
    y = bg * depthwise_conv(cg * u, conv_w, conv_b)
    return (y * jax.nn.silu(z)) @ w_out


def setup_inputs(seed: int = 0) -> dict:
    key = jax.random.key(seed)
    ks = jax.random.split(key, 24)

    def nrm(k, shape, scale):
        return jax.random.normal(k, shape, jnp.float32) * scale

    return {
        "x": nrm(ks[0], (BATCH, SEQ, D_MODEL), 1.0),
        "c": nrm(ks[1], (BATCH, D_MODEL), 1.0),
        "ctx": nrm(ks[2], (BATCH, CTX_LEN, D_MODEL), 1.0),
        "c_ctx": nrm(ks[3], (D_MODEL,), 1.0),
        "norm_g": 1.0 + nrm(ks[4], (DEPTH, D_MODEL), 0.02),
        "ada_w": nrm(ks[5], (DEPTH, D_MODEL, 3 * D_MODEL), D_MODEL ** -0.5),
        "ada_b": nrm(ks[6], (DEPTH, 3 * D_MODEL), 0.02),
        "ev_w_in": nrm(ks[7], (N_EVEN, D_MODEL, EV_COLS), D_MODEL ** -0.5),
        "ev_dw_w": nrm(ks[8], (N_EVEN, CONF_WIDTH, D_CONV), CONF_WIDTH ** -0.5),
        "ev_dw_b": nrm(ks[9], (N_EVEN, D_CONV), 0.02),
        "ev_ln_g": 1.0 + nrm(ks[10], (N_EVEN, D_CONV), 0.02),
        "ev_ln_b": nrm(ks[11], (N_EVEN, D_CONV), 0.02),
        "ev_sink": nrm(ks[12], (N_EVEN, N_Q_HEADS), 0.5),
        "ev_w_out": nrm(ks[13], (N_EVEN, D_INNER, D_MODEL), D_INNER ** -0.5),
        "od_w_in": nrm(ks[14], (N_ODD, D_MODEL, OD_COLS), D_MODEL ** -0.5),
        "od_conv_w": nrm(ks[15], (N_ODD, SC_WIDTH, D_INNER), SC_WIDTH ** -0.5),
        "od_conv_b": nrm(ks[16], (N_ODD, D_INNER), 0.02),
        "od_w_out": nrm(ks[17], (N_ODD, D_INNER, D_MODEL), D_INNER ** -0.5),
        "final_g": 1.0 + nrm(ks[18], (D_MODEL,), 0.02),
    }


def reference(x, c, ctx, c_ctx, norm_g, ada_w, ada_b, ev_w_in, ev_dw_w, ev_dw_b, ev_ln_g,
              ev_ln_b, ev_sink, ev_w_out, od_w_in, od_conv_w, od_conv_b, od_w_out, final_g):
    bsz, s_len, _ = x.shape
    n_ctx = ctx.shape[1]
    rows_n = s_len // GRID_W
    row = jnp.broadcast_to(jnp.arange(rows_n)[:, None], (rows_n, GRID_W)).reshape(-1).astype(jnp.float32)
    col = jnp.broadcast_to(jnp.arange(GRID_W)[None, :], (rows_n, GRID_W)).reshape(-1).astype(jnp.float32)

    for i in range(DEPTH):
        g = norm_g[i]
        sh, sc, gt = adaln(c, ada_w[i], ada_b[i])
        xn = modulate(rms_norm(x, g), sh[:, None, :], sc[:, None, :])
        ctx_out_needed = any(j % 2 == 0 for j in range(i + 1, DEPTH))

        if i % 2 == 0:
            e = i // 2
            w_in = ev_w_in[e]
            csh, csc, cgt = adaln(c_ctx, ada_w[i], ada_b[i])
            cn = modulate(rms_norm(ctx, g), csh, csc)
            if ctx_out_needed:
                hc = cn @ w_in
                hc_kv = hc[..., EV_K:EV_G]
            else:
                hc_kv = cn @ w_in[:, EV_K:EV_G]
            kc = split_heads(hc_kv[..., :N_KV_HEADS * HEAD_DIM], N_KV_HEADS)
            vc = split_heads(hc_kv[..., N_KV_HEADS * HEAD_DIM:], N_KV_HEADS)

            h = xn @ w_in
            a = conformer_conv(h[..., EV_A:D_CONV], h[..., D_CONV:EV_Q],
                               ev_dw_w[e], ev_dw_b[e], ev_ln_g[e], ev_ln_b[e])
            q = axial_rope(split_heads(h[..., EV_Q:EV_K], N_Q_HEADS), row, col)
            q = q.reshape(bsz, s_len, N_KV_HEADS, GQA_GROUP, HEAD_DIM)
            k = axial_rope(split_heads(h[..., EV_K:EV_V], N_KV_HEADS), row, col)
            v = split_heads(h[..., EV_V:EV_G], N_KV_HEADS)
            att = latent_window_attention(q, k, v, kc, vc, ev_sink[e])
            y = (jnp.concatenate([a, att], axis=-1) * jax.nn.silu(h[..., EV_G:])) @ ev_w_out[e]
            x = x + gt[:, None, :] * y

            if ctx_out_needed:
                a_c = conformer_conv(hc[..., EV_A:D_CONV], hc[..., D_CONV:EV_Q],
                                     ev_dw_w[e], ev_dw_b[e], ev_ln_g[e], ev_ln_b[e])
                qc = split_heads(hc[..., EV_Q:EV_K], N_Q_HEADS).reshape(
                    bsz, n_ctx, N_KV_HEADS, GQA_GROUP, HEAD_DIM)
                att_c = context_attention(qc, kc, vc, ev_sink[e])
                yc = (jnp.concatenate([a_c, att_c], axis=-1) * jax.nn.silu(hc[..., EV_G:])) @ ev_w_out[e]
                ctx = ctx + cgt * yc
        else:
            o = i // 2
            h = xn @ od_w_in[o]
            x = x + gt[:, None, :] * short_gated_conv(h, od_conv_w[o], od_conv_b[o], od_w_out[o])
            if ctx_out_needed:
                csh, csc, cgt = adaln(c_ctx, ada_w[i], ada_b[i])
                cn = modulate(rms_norm(ctx, g), csh, csc)
                hc = cn @ od_w_in[o]
                ctx = ctx + cgt * short_gated_conv(hc, od_conv_w[o], od_conv_b[o], od_w_out[o])

    return rms_norm(x, final_g)
```

```python
import functools

import jax
import jax.numpy as jnp
from jax import lax
from jax.experimental import pallas as pl
from jax.experimental.pallas import tpu as pltpu

F32 = jnp.float32
BF16 = jnp.bfloat16

D_MODEL = 1024
DEPTH = 4
GRID_W = 64
EPS = 1e-6
D_INNER = 2048
D_CONV = 1024
HEAD_DIM = 64
N_Q_HEADS = 16
N_KV_HEADS = 4
CONF_WIDTH = 31
WINDOW = 128
ROPE_THETA = 10000.0
NEG_INF = -1e30
EV_Q = 2 * D_CONV
EV_K = EV_Q + N_Q_HEADS * HEAD_DIM
EV_V = EV_K + N_KV_HEADS * HEAD_DIM
EV_G = EV_V + N_KV_HEADS * HEAD_DIM
EV_COLS = EV_G + D_INNER

LANES = 128
SUBLANES = 8
VMEM_LIMIT_BYTES = 56 * 1024 * 1024

Q_BLOCK = 128
HALO = 16
CONV_ROWS = 32
CONV_LANES = 256
OD_HALO = 8
OD_CHUNK = 512


def _sigmoid(v):
    return 1.0 / (1.0 + jnp.exp(-v))


def _rms_mod(x, g, shift, scale):
    y = x * lax.rsqrt(jnp.mean(x * x, axis=-1, keepdims=True) + EPS) * g
    return y * (1.0 + scale) + shift


def _const_spec(shape):
    nd = len(shape)
    return pl.BlockSpec(shape, lambda *_: (0,) * nd, pipeline_mode=pl.Buffered(1))


def _params(n_axes):
    return pltpu.CompilerParams(
        dimension_semantics=("arbitrary",) * n_axes,
        vmem_limit_bytes=VMEM_LIMIT_BYTES)


def _adaln_kernel(cond_ref, w_ref, b_ref, o_ref):
    cnd = cond_ref[...]
    a = (cnd * _sigmoid(cnd)).astype(BF16)
    o_ref[0] = jnp.dot(a, w_ref[0].astype(BF16), preferred_element_type=F32) + b_ref[0]


def _adaln_all(cond, ada_w, ada_b):
    rows = cond.shape[0]
    tn = 1024
    return pl.pallas_call(
        _adaln_kernel,
        grid=(DEPTH, 3 * D_MODEL // tn),
        in_specs=[
            pl.BlockSpec((rows, D_MODEL), lambda i, j: (0, 0)),
            pl.BlockSpec((1, D_MODEL, tn), lambda i, j: (i, 0, j)),
            pl.BlockSpec((1, 1, tn), lambda i, j: (i, 0, j)),
        ],
        out_specs=pl.BlockSpec((1, rows, tn), lambda i, j: (i, 0, j)),
        out_shape=jax.ShapeDtypeStruct((DEPTH, rows, 3 * D_MODEL), F32),
        compiler_params=_params(2),
        name="adaln",
    )(cond, ada_w, ada_b.reshape(DEPTH, 1, 3 * D_MODEL))


def _rope_slab(t, cos, sin, first_half):
    partner = jnp.where(first_half, pltpu.roll(t, LANES - 16, 1), pltpu.roll(t, 16, 1))
    return t * cos + partner * sin


def _dup_heads(s, low_half):
    r = pltpu.roll(s, HEAD_DIM, 1)
    return jnp.where(low_half, s, r), jnp.where(low_half, r, s)


def _ev_in_kernel(*refs, rope, kv_only):
    refs = list(refs)
    x_ref, mod_ref, g_ref, w_ref = refs[:4]
    pos = 4
    if rope:
        cos_ref, sin_ref = refs[pos:pos + 2]
        pos += 2
    if kv_only:
        kd_ref, vd_ref = refs[pos:pos + 2]
        k_col = 0
    else:
        u_ref, q_ref, kd_ref, vd_ref, gz_ref = refs[pos:pos + 5]
        k_col = EV_K
    v_col = k_col + N_KV_HEADS * HEAD_DIM

    tm = x_ref.shape[1]
    xn = _rms_mod(x_ref[0], g_ref[...], mod_ref[0, 0:1, :], mod_ref[0, 1:2, :]).astype(BF16)

    def proj(c0, width):
        return jnp.dot(xn, w_ref[:, c0:c0 + width], preferred_element_type=F32)

    lane = lax.broadcasted_iota(jnp.int32, (tm, LANES), 1)
    low_half = lane < HEAD_DIM
    first_half = (lane & 31) < 16
    if rope:
        cos = cos_ref[...]
        sin = sin_ref[...]

    if not kv_only:
        for j in range(D_CONV // 512):
            val = proj(512 * j, 512)
            gate = proj(D_CONV + 512 * j, 512)
            u_ref[0, :, 512 * j:512 * (j + 1)] = val * _sigmoid(gate)
        for j in range(N_Q_HEADS * HEAD_DIM // 512):
            t = proj(EV_Q + 512 * j, 512)
            for s in range(512 // LANES):
                slab = t[:, LANES * s:LANES * (s + 1)]
                if rope:
                    slab = _rope_slab(slab, cos, sin, first_half)
                c0 = 512 * j + LANES * s
                q_ref[0, :, c0:c0 + LANES] = (slab * (HEAD_DIM ** -0.5)).astype(BF16)
        for j in range(D_INNER // 512):
            z = proj(EV_G + 512 * j, 512)
            gz_ref[0, :, 512 * j:512 * (j + 1)] = (z * _sigmoid(z)).astype(BF16)

    kk = proj(k_col, N_KV_HEADS * HEAD_DIM)
    vv = proj(v_col, N_KV_HEADS * HEAD_DIM)
    for s in range(N_KV_HEADS * HEAD_DIM // LANES):
        ks = kk[:, LANES * s:LANES * (s + 1)]
        if rope:
            ks = _rope_slab(ks, cos, sin, first_half)
        k0, k1 = _dup_heads(ks, low_half)
        kd_ref[0, :, 2 * LANES * s:2 * LANES * s + LANES] = k0.astype(BF16)
        kd_ref[0, :, 2 * LANES * s + LANES:2 * LANES * (s + 1)] = k1.astype(BF16)
        v0, v1 = _dup_heads(vv[:, LANES * s:LANES * (s + 1)], low_half)
        vd_ref[0, :, 2 * LANES * s:2 * LANES * s + LANES] = v0.astype(BF16)
        vd_ref[0, :, 2 * LANES * s + LANES:2 * LANES * (s + 1)] = v1.astype(BF16)


def _ev_in(x, mod, g, w, cos_sin, *, kv_only, tm):
    bsz, s_len, _ = x.shape
    rope = cos_sin is not None
    tok = lambda width: pl.BlockSpec((1, tm, width), lambda b, t: (b, t, 0))
    in_specs = [
        tok(D_MODEL),
        pl.BlockSpec((1, 3, D_MODEL), lambda b, t: (b, 0, 0)),
        _const_spec((1, D_MODEL)),
        _const_spec(w.shape),
    ]
    args = [x, mod, g, w]
    if rope:
        in_specs += [pl.BlockSpec((tm, LANES), lambda b, t: (t, 0))] * 2
        args += list(cos_sin)
    kv_w = 2 * N_KV_HEADS * HEAD_DIM
    shp = lambda width, dt: jax.ShapeDtypeStruct((bsz, s_len, width), dt)
    if kv_only:
        out_specs = [tok(kv_w), tok(kv_w)]
        out_shape = [shp(kv_w, BF16), shp(kv_w, BF16)]
    else:
        out_specs = [tok(D_CONV), tok(N_Q_HEADS * HEAD_DIM), tok(kv_w), tok(kv_w), tok(D_INNER)]
        out_shape = [shp(D_CONV, F32), shp(N_Q_HEADS * HEAD_DIM, BF16), shp(kv_w, BF16),
                     shp(kv_w, BF16), shp(D_INNER, BF16)]
    return pl.pallas_call(
        functools.partial(_ev_in_kernel, rope=rope, kv_only=kv_only),
        grid=(bsz, s_len // tm),
        in_specs=in_specs,
        out_specs=out_specs,
        out_shape=out_shape,
        compiler_params=_params(2),
        name="ev_in_kv" if kv_only else "ev_in",
    )(*args)


def _attn_kernel(*refs, has_local):
    refs = list(refs)
    sink_ref, q_ref = refs[:2]
    pos = 2
    if has_local:
        kp_ref, ks_ref, kn_ref, vp_ref, vs_ref, vn_ref = refs[pos:pos + 6]
        pos += 6
    kc_ref, vc_ref, o_ref = refs[pos:pos + 3]

    qb = q_ref.shape[1]
    qi = pl.program_id(1)
    nq = pl.num_programs(1)
    lane = lax.broadcasted_iota(jnp.int32, (qb, LANES), 1)
    low_half = lane < HEAD_DIM
    row2 = lax.broadcasted_iota(jnp.int32, (2 * qb, 1), 0)
    if has_local:
        r = lax.broadcasted_iota(jnp.int32, (2 * qb, qb), 0) & (qb - 1)
        c = lax.broadcasted_iota(jnp.int32, (2 * qb, qb), 1)
        mask_prev = jnp.logical_and(c >= r, qi > 0)
        mask_next = jnp.logical_and(c <= r, qi < nq - 1)

    def qk(a, b):
        return lax.dot_general(a, b, (((1,), (1,)), ((), ())), preferred_element_type=F32)

    def pv(p, v):
        return jnp.dot(p.astype(BF16), v, preferred_element_type=F32)

    for h in range(N_KV_HEADS):
        hs = slice(LANES * h, LANES * (h + 1))
        kc = kc_ref[0, :, hs]
        vc = vc_ref[0, :, hs]
        if has_local:
            kp, ks, kn = kp_ref[0, :, hs], ks_ref[0, :, hs], kn_ref[0, :, hs]
            vp, vs, vn = vp_ref[0, :, hs], vs_ref[0, :, hs], vn_ref[0, :, hs]
        for pr in range(2):
            c0 = 2 * LANES * h + LANES * pr
            qp = q_ref[0, :, c0:c0 + LANES]
            zero = jnp.zeros_like(qp)
            qq = jnp.concatenate([jnp.where(low_half, qp, zero),
                                  jnp.where(low_half, zero, qp)], axis=0)
            sink_col = jnp.where(row2 < qb, sink_ref[4 * h + 2 * pr], sink_ref[4 * h + 2 * pr + 1])
            s_c = qk(qq, kc)
            m = jnp.maximum(jnp.max(s_c, axis=-1, keepdims=True), sink_col)
            if has_local:
                s_p = jnp.where(mask_prev, qk(qq, kp), NEG_INF)
                s_s = qk(qq, ks)
                s_n = jnp.where(mask_next, qk(qq, kn), NEG_INF)
                m = jnp.maximum(m, jnp.max(s_p, axis=-1, keepdims=True))
                m = jnp.maximum(m, jnp.max(s_s, axis=-1, keepdims=True))
                m = jnp.maximum(m, jnp.max(s_n, axis=-1, keepdims=True))
            p_c = jnp.exp(s_c - m)
            l = jnp.sum(p_c, axis=-1, keepdims=True) + jnp.exp(sink_col - m)
            o = pv(p_c, vc)
            if has_local:
                p_p = jnp.exp(s_p - m)
                p_s = jnp.exp(s_s - m)
                p_n = jnp.exp(s_n - m)
                l = (l + jnp.sum(p_p, axis=-1, keepdims=True) + jnp.sum(p_s, axis=-1, keepdims=True)
                     + jnp.sum(p_n, axis=-1, keepdims=True))
                o = o + pv(p_p, vp) + pv(p_s, vs) + pv(p_n, vn)
            o = o * (1.0 / l)
            o_ref[0, :, c0:c0 + LANES] = jnp.where(low_half, o[:qb], o[qb:]).astype(BF16)


def _attention(q, kd, vd, kcd, vcd, sink, *, has_local):
    bsz, s_len, qw = q.shape
    n_ctx = kcd.shape[1]
    kv_w = kd.shape[2] if has_local else kcd.shape[2]
    nq = s_len // Q_BLOCK
    in_specs = [
        pl.BlockSpec(memory_space=pltpu.SMEM),
        pl.BlockSpec((1, Q_BLOCK, qw), lambda b, i: (b, i, 0)),
    ]
    args = [sink, q]
    if has_local:
        prev = pl.BlockSpec((1, Q_BLOCK, kv_w), lambda b, i: (b, jnp.maximum(i - 1, 0), 0))
        this = pl.BlockSpec((1, Q_BLOCK, kv_w), lambda b, i: (b, i, 0))
        nxt = pl.BlockSpec((1, Q_BLOCK, kv_w), lambda b, i: (b, jnp.minimum(i + 1, nq - 1), 0))
        in_specs += [prev, this, nxt, prev, this, nxt]
        args += [kd, kd, kd, vd, vd, vd]
    cspec = pl.BlockSpec((1, n_ctx, kv_w), lambda b, i: (b, 0, 0))
    in_specs += [cspec, cspec]
    args += [kcd, vcd]
    return pl.pallas_call(
        functools.partial(_attn_kernel, has_local=has_local),
        grid=(bsz, nq),
        in_specs=in_specs,
        out_specs=pl.BlockSpec((1, Q_BLOCK, qw), lambda b, i: (b, i, 0)),
        out_shape=jax.ShapeDtypeStruct((bsz, s_len, qw), BF16),
        compiler_params=_params(2),
        name="attn_local" if has_local else "attn_ctx",
    )(*args)


def _ev_out_kernel(u_ref, up_ref, un_ref, att_ref, gz_ref, x_ref, mod_ref, dww_ref, dwb_ref,
                   lng_ref, lnb_ref, wout_ref, o_ref, uext, ush, conv):
    tm = u_ref.shape[1]
    ext = tm + 2 * HALO
    t = pl.program_id(1)
    nt = pl.num_programs(1)

    uext[0:HALO] = jnp.where(t > 0, up_ref[0], 0.0)
    uext[HALO:HALO + tm] = u_ref[0]
    uext[HALO + tm:ext] = jnp.where(t < nt - 1, un_ref[0], 0.0)

    for c in range(D_CONV // CONV_LANES):
        cs = slice(CONV_LANES * c, CONV_LANES * (c + 1))
        for r in range(SUBLANES):
            ush[r] = uext[r:r + ext - SUBLANES, cs]

        def rows(i, carry):
            r0 = pl.multiple_of(i * CONV_ROWS, CONV_ROWS)
            acc = jnp.broadcast_to(dwb_ref[:, cs], (CONV_ROWS, CONV_LANES))
            for k in range(CONF_WIDTH):
                a, r = divmod(k + HALO - CONF_WIDTH // 2, SUBLANES)
                acc = acc + dww_ref[k:k + 1, cs] * ush[r, pl.ds(r0 + SUBLANES * a, CONV_ROWS), :]
            conv[pl.ds(r0, CONV_ROWS), cs] = acc
            return carry

        lax.fori_loop(0, tm // CONV_ROWS, rows, 0)

    cv = conv[...]
    mu = jnp.mean(cv, axis=-1, keepdims=True)
    d = cv - mu
    var = jnp.mean(d * d, axis=-1, keepdims=True)
    y = d * lax.rsqrt(var + EPS) * lng_ref[...] + lnb_ref[...]
    a = y * _sigmoid(y)

    gz = gz_ref[0].astype(F32)
    act_a = (a * gz[:, :D_CONV]).astype(BF16)
    act_b = (att_ref[0].astype(F32) * gz[:, D_CONV:]).astype(BF16)
    yo = (jnp.dot(act_a, wout_ref[0:D_CONV, :], preferred_element_type=F32)
          + jnp.dot(act_b, wout_ref[D_CONV:D_INNER, :], preferred_element_type=F32))
    o_ref[0] = x_ref[0] + mod_ref[0, 2:3, :] * yo


def _ev_out(u, att, gz, x, mod, dw_w, dw_b, ln_g, ln_b, w_out, *, tm):
    bsz, s_len, _ = x.shape
    nh = tm // HALO
    n_halo_blocks = s_len // HALO
    tok = lambda width: pl.BlockSpec((1, tm, width), lambda b, t: (b, t, 0))
    in_specs = [
        tok(D_CONV),
        pl.BlockSpec((1, HALO, D_CONV), lambda b, t: (b, jnp.maximum(t * nh - 1, 0), 0)),
        pl.BlockSpec((1, HALO, D_CONV),
                     lambda b, t: (b, jnp.minimum((t + 1) * nh, n_halo_blocks - 1), 0)),
        tok(D_CONV),
        tok(D_INNER),
        tok(D_MODEL),
        pl.BlockSpec((1, 3, D_MODEL), lambda b, t: (b, 0, 0)),
        _const_spec((CONF_WIDTH, D_CONV)),
        _const_spec((1, D_CONV)),
        _const_spec((1, D_CONV)),
        _const_spec((1, D_CONV)),
        _const_spec((D_INNER, D_MODEL)),
    ]
    ext = tm + 2 * HALO
    return pl.pallas_call(
        _ev_out_kernel,
        grid=(bsz, s_len // tm),
        in_specs=in_specs,
        out_specs=tok(D_MODEL),
        out_shape=jax.ShapeDtypeStruct(x.shape, F32),
        scratch_shapes=[
            pltpu.VMEM((ext, D_CONV), F32),
            pltpu.VMEM((SUBLANES, ext - SUBLANES, CONV_LANES), F32),
            pltpu.VMEM((tm, D_CONV), F32),
        ],
        compiler_params=_params(2),
        name="ev_out",
    )(u, u, u, att, gz, x, mod, dw_w, dw_b, ln_g, ln_b, w_out)


def _od_kernel(*refs, final):
    refs = list(refs)
    x_ref, xp_ref, xn_ref, mod_ref, g_ref, win_ref, cw_ref, cb_ref, wout_ref = refs[:9]
    if final:
        fg_ref, o_ref = refs[9:11]
    else:
        o_ref = refs[9]
    tm = x_ref.shape[1]
    ext = tm + 2 * OD_HALO
    t = pl.program_id(1)
    nt = pl.num_programs(1)

    xe = jnp.concatenate([xp_ref[0], x_ref[0], xn_ref[0]], axis=0)
    xn = _rms_mod(xe, g_ref[...], mod_ref[0, 0:1, :], mod_ref[0, 1:2, :]).astype(BF16)

    row = lax.broadcasted_iota(jnp.int32, (ext, 1), 0)
    inside = jnp.logical_and(jnp.logical_or(row >= OD_HALO, t > 0),
                             jnp.logical_or(row < OD_HALO + tm, t < nt - 1))

    acc = jnp.zeros((tm, D_MODEL), F32)
    for j in range(D_INNER // OD_CHUNK):
        def proj(sec):
            c0 = sec * D_INNER + OD_CHUNK * j
            return jnp.dot(xn, win_ref[:, c0:c0 + OD_CHUNK], preferred_element_type=F32)
        cs = slice(OD_CHUNK * j, OD_CHUNK * (j + 1))
        bg, cg, uu, z = proj(0), proj(1), proj(2), proj(3)
        p = jnp.where(inside, cg * uu, 0.0)
        p_prev = pltpu.roll(p, 1, 0)
        p_next = pltpu.roll(p, ext - 1, 0)
        cv = (cw_ref[0:1, cs] * p_prev + cw_ref[1:2, cs] * p + cw_ref[2:3, cs] * p_next
              + cb_ref[:, cs])
        ya = ((bg * cv) * (z * _sigmoid(z)))[OD_HALO:OD_HALO + tm].astype(BF16)
        acc = acc + jnp.dot(ya, wout_ref[cs, :], preferred_element_type=F32)

    xo = x_ref[0] + mod_ref[0, 2:3, :] * acc
    if final:
        xo = xo * lax.rsqrt(jnp.mean(xo * xo, axis=-1, keepdims=True) + EPS) * fg_ref[...]
    o_ref[0] = xo


def _od_layer(x, mod, g, w_in, conv_w, conv_b, w_out, final_g, *, tm):
    bsz, s_len, _ = x.shape
    nh = tm // OD_HALO
    n_halo_blocks = s_len // OD_HALO
    final = final_g is not None
    tok = pl.BlockSpec((1, tm, D_MODEL), lambda b, t: (b, t, 0))
    in_specs = [
        tok,
        pl.BlockSpec((1, OD_HALO, D_MODEL), lambda b, t: (b, jnp.maximum(t * nh - 1, 0), 0)),
        pl.BlockSpec((1, OD_HALO, D_MODEL),
                     lambda b, t: (b, jnp.minimum((t + 1) * nh, n_halo_blocks - 1), 0)),
        pl.BlockSpec((1, 3, D_MODEL), lambda b, t: (b, 0, 0)),
        _const_spec((1, D_MODEL)),
        _const_spec(w_in.shape),
        _const_spec(conv_w.shape),
        _const_spec((1, D_INNER)),
        _const_spec(w_out.shape),
    ]
    args = [x, x, x, mod, g, w_in, conv_w, conv_b, w_out]
    if final:
        in_specs.append(_const_spec((1, D_MODEL)))
        args.append(final_g)
    return pl.pallas_call(
        functools.partial(_od_kernel, final=final),
        grid=(bsz, s_len // tm),
        in_specs=in_specs,
        out_specs=tok,
        out_shape=jax.ShapeDtypeStruct(x.shape, F32),
        compiler_params=_params(2),
        name="od_layer_final" if final else "od_layer",
    )(*args)


def _rope_tables(s_len):
    pos = jnp.arange(s_len)
    row = (pos // GRID_W).astype(F32)
    col = (pos % GRID_W).astype(F32)
    quarter = HEAD_DIM // 4
    freqs = ROPE_THETA ** (-jnp.arange(quarter, dtype=F32) / quarter)
    d = jnp.arange(LANES) % HEAD_DIM
    p = jnp.where((d >= HEAD_DIM // 2)[None, :], col[:, None], row[:, None])
    ang = p * freqs[d % quarter][None, :]
    sign = jnp.where((d % (HEAD_DIM // 2)) < quarter, -1.0, 1.0).astype(F32)
    return jnp.cos(ang), jnp.sin(ang) * sign[None, :]


def kernel(x, c, ctx, c_ctx, norm_g, ada_w, ada_b, ev_w_in, ev_dw_w, ev_dw_b, ev_ln_g, ev_ln_b,
           ev_sink, ev_w_out, od_w_in, od_conv_w, od_conv_b, od_w_out, final_g):
    bsz, s_len, _ = x.shape
    n_ctx = ctx.shape[1]
    assert s_len % 512 == 0 and n_ctx % Q_BLOCK == 0 and bsz + 1 <= 16

    cond = jnp.zeros((16, D_MODEL), F32).at[:bsz].set(c).at[bsz].set(c_ctx)
    mods = _adaln_all(cond, ada_w, ada_b)
    cos_sin = _rope_tables(s_len)

    for i in range(DEPTH):
        mod_x = mods[i, :bsz].reshape(bsz, 3, D_MODEL)
        mod_c = jnp.broadcast_to(mods[i, bsz].reshape(1, 3, D_MODEL), (bsz, 3, D_MODEL))
        g = norm_g[i].reshape(1, D_MODEL)
        ctx_out_needed = any(j % 2 == 0 for j in range(i + 1, DEPTH))
        if i % 2 == 0:
            e = i // 2
            w_in = ev_w_in[e].astype(BF16)
            w_out = ev_w_out[e].astype(BF16)
            conv_args = (ev_dw_w[e], ev_dw_b[e].reshape(1, D_CONV), ev_ln_g[e].reshape(1, D_CONV),
                         ev_ln_b[e].reshape(1, D_CONV), w_out)
            if ctx_out_needed:
                uc, qc, kcd, vcd, gzc = _ev_in(ctx, mod_c, g, w_in, None, kv_only=False, tm=n_ctx)
            else:
                kcd, vcd = _ev_in(ctx, mod_c, g, w_in[:, EV_K:EV_G], None, kv_only=True, tm=n_ctx)
            u, q, kd, vd, gz = _ev_in(x, mod_x, g, w_in, cos_sin, kv_only=False, tm=512)
            att = _attention(q, kd, vd, kcd, vcd, ev_sink[e], has_local=True)
            x = _ev_out(u, att, gz, x, mod_x, *conv_args, tm=256)
            if ctx_out_needed:
                att_c = _attention(qc, None, None, kcd, vcd, ev_sink[e], has_local=False)
                ctx = _ev_out(uc, att_c, gzc, ctx, mod_c, *conv_args, tm=n_ctx)
        else:
            o = i // 2
            od_args = (od_w_in[o].astype(BF16), od_conv_w[o], od_conv_b[o].reshape(1, D_INNER),
                       od_w_out[o].astype(BF16))
            fg = final_g.reshape(1, D_MODEL) if i == DEPTH - 1 else None
            x = _od_layer(x, mod_x, g, *od_args, fg, tm=512)
            if ctx_out_needed:
                ctx = _od_layer(ctx, mod_c, g, *od_args, None, tm=n_ctx)
    return x
```

```python
import functools

import jax
import jax.numpy as jnp
from jax import lax
from jax.experimental import pallas as pl
from jax.experimental.pallas import tpu as pltpu

F32 = jnp.float32
BF16 = jnp.bfloat16

D_MODEL = 1024
DEPTH = 4
GRID_W = 64
EPS = 1e-6
D_INNER = 2048
D_CONV = 1024
HEAD_DIM = 64
N_Q_HEADS = 16
N_KV_HEADS = 4
CONF_WIDTH = 31
WINDOW = 128
ROPE_THETA = 10000.0
NEG_INF = -1e30
EV_Q = 2 * D_CONV
EV_K = EV_Q + N_Q_HEADS * HEAD_DIM
EV_V = EV_K + N_KV_HEADS * HEAD_DIM
EV_G = EV_V + N_KV_HEADS * HEAD_DIM
EV_COLS = EV_G + D_INNER

LANES = 128
SUBLANES = 8
VMEM_LIMIT_BYTES = 56 * 1024 * 1024

Q_BLOCK = 128
HALO = 16
CONV_ROWS = 32
CONV_LANES = 256
OD_HALO = 8
OD_CHUNK = 512


def _sigmoid(v):
    return 1.0 / (1.0 + jnp.exp(-v))


def _rms_mod(x, g, shift, scale):
    y = x * lax.rsqrt(jnp.mean(x * x, axis=-1, keepdims=True) + EPS) * g
    return y * (1.0 + scale) + shift


def _const_spec(shape):
    nd = len(shape)
    return pl.BlockSpec(shape, lambda *_: (0,) * nd, pipeline_mode=pl.Buffered(1))


def _params(n_axes):
    return pltpu.CompilerParams(
        dimension_semantics=("arbitrary",) * n_axes,
        vmem_limit_bytes=VMEM_LIMIT_BYTES)


def _adaln_kernel(cond_ref, w_ref, b_ref, o_ref):
    cnd = cond_ref[...]
    a = (cnd * _sigmoid(cnd)).astype(BF16)
    o_ref[0] = jnp.dot(a, w_ref[0].astype(BF16), preferred_element_type=F32) + b_ref[0]


def _adaln_all(cond, ada_w, ada_b):
    rows = cond.shape[0]
    tn = 1024
    return pl.pallas_call(
        _adaln_kernel,
        grid=(DEPTH, 3 * D_MODEL // tn),
        in_specs=[
            pl.BlockSpec((rows, D_MODEL), lambda i, j: (0, 0)),
            pl.BlockSpec((1, D_MODEL, tn), lambda i, j: (i, 0, j)),
            pl.BlockSpec((1, 1, tn), lambda i, j: (i, 0, j)),
        ],
        out_specs=pl.BlockSpec((1, rows, tn), lambda i, j: (i, 0, j)),
        out_shape=jax.ShapeDtypeStruct((DEPTH, rows, 3 * D_MODEL), F32),
        compiler_params=_params(2),
        name="adaln",
    )(cond, ada_w, ada_b.reshape(DEPTH, 1, 3 * D_MODEL))


def _rope_slab(t, cos, sin, first_half):
    partner = jnp.where(first_half, pltpu.roll(t, LANES - 16, 1), pltpu.roll(t, 16, 1))
    return t * cos + partner * sin


def _dup_heads(s, low_half):
    r = pltpu.roll(s, HEAD_DIM, 1)
    return jnp.where(low_half, s, r), jnp.where(low_half, r, s)


def _ev_in_kernel(*refs, rope, kv_only):
    refs = list(refs)
    x_ref, mod_ref, g_ref, w_ref = refs[:4]
    pos = 4
    if rope:
        cos_ref, sin_ref = refs[pos:pos + 2]
        pos += 2
    if kv_only:
        kd_ref, vd_ref = refs[pos:pos + 2]
        k_col = 0
    else:
        u_ref, q_ref, kd_ref, vd_ref, gz_ref = refs[pos:pos + 5]
        k_col = EV_K
    v_col = k_col + N_KV_HEADS * HEAD_DIM

    tm = x_ref.shape[1]
    xn = _rms_mod(x_ref[0], g_ref[...], mod_ref[0, 0:1, :], mod_ref[0, 1:2, :]).astype(BF16)

    def proj(c0, width):
        return jnp.dot(xn, w_ref[:, c0:c0 + width], preferred_element_type=F32)

    lane = lax.broadcasted_iota(jnp.int32, (tm, LANES), 1)
    low_half = lane < HEAD_DIM
    first_half = (lane & 31) < 16
    if rope:
        cos = cos_ref[...]
        sin = sin_ref[...]

    if not kv_only:
        for j in range(D_CONV // 512):
            val = proj(512 * j, 512)
            gate = proj(D_CONV + 512 * j, 512)
            u_ref[0, :, 512 * j:512 * (j + 1)] = val * _sigmoid(gate)
        for j in range(N_Q_HEADS * HEAD_DIM // 512):
            t = proj(EV_Q + 512 * j, 512)
            for s in range(512 // LANES):
                slab = t[:, LANES * s:LANES * (s + 1)]
                if rope:
                    slab = _rope_slab(slab, cos, sin, first_half)
                c0 = 512 * j + LANES * s
                q_ref[0, :, c0:c0 + LANES] = (slab * (HEAD_DIM ** -0.5)).astype(BF16)
        for j in range(D_INNER // 512):
            z = proj(EV_G + 512 * j, 512)
            gz_ref[0, :, 512 * j:512 * (j + 1)] = (z * _sigmoid(z)).astype(BF16)

    kk = proj(k_col, N_KV_HEADS * HEAD_DIM)
    vv = proj(v_col, N_KV_HEADS * HEAD_DIM)
    for s in range(N_KV_HEADS * HEAD_DIM // LANES):
        ks = kk[:, LANES * s:LANES * (s + 1)]
        if rope:
            ks = _rope_slab(ks, cos, sin, first_half)
        k0, k1 = _dup_heads(ks, low_half)
        kd_ref[0, :, 2 * LANES * s:2 * LANES * s + LANES] = k0.astype(BF16)
        kd_ref[0, :, 2 * LANES * s + LANES:2 * LANES * (s + 1)] = k1.astype(BF16)
        v0, v1 = _dup_heads(vv[:, LANES * s:LANES * (s + 1)], low_half)
        vd_ref[0, :, 2 * LANES * s:2 * LANES * s + LANES] = v0.astype(BF16)
        vd_ref[0, :, 2 * LANES * s + LANES:2 * LANES * (s + 1)] = v1.astype(BF16)


def _ev_in(x, mod, g, w, cos_sin, *, kv_only, tm):
    bsz, s_len, _ = x.shape
    rope = cos_sin is not None
    tok = lambda width: pl.BlockSpec((1, tm, width), lambda b, t: (b, t, 0))
    in_specs = [
        tok(D_MODEL),
        pl.BlockSpec((1, 3, D_MODEL), lambda b, t: (b, 0, 0)),
        _const_spec((1, D_MODEL)),
        _const_spec(w.shape),
    ]
    args = [x, mod, g, w]
    if rope:
        in_specs += [pl.BlockSpec((tm, LANES), lambda b, t: (t, 0))] * 2
        args += list(cos_sin)
    kv_w = 2 * N_KV_HEADS * HEAD_DIM
    shp = lambda width, dt: jax.ShapeDtypeStruct((bsz, s_len, width), dt)
    if kv_only:
        out_specs = [tok(kv_w), tok(kv_w)]
        out_shape = [shp(kv_w, BF16), shp(kv_w, BF16)]
    else:
        out_specs = [tok(D_CONV), tok(N_Q_HEADS * HEAD_DIM), tok(kv_w), tok(kv_w), tok(D_INNER)]
        out_shape = [shp(D_CONV, F32), shp(N_Q_HEADS * HEAD_DIM, BF16), shp(kv_w, BF16),
                     shp(kv_w, BF16), shp(D_INNER, BF16)]
    return pl.pallas_call(
        functools.partial(_ev_in_kernel, rope=rope, kv_only=kv_only),
        grid=(bsz, s_len // tm),
        in_specs=in_specs,
        out_specs=out_specs,
        out_shape=out_shape,
        compiler_params=_params(2),
        name="ev_in_kv" if kv_only else "ev_in",
    )(*args)


def _mixer_kernel(*refs, has_local):
    refs = list(refs)
    sink_ref, q_ref = refs[:2]
    pos = 2
    if has_local:
        kp_ref, ks_ref, kn_ref, vp_ref, vs_ref, vn_ref = refs[pos:pos + 6]
        pos += 6
    (kc_ref, vc_ref, u_ref, up_ref, un_ref, gz_ref, dww_ref, dwb_ref, lng_ref, lnb_ref,
     act_ref, uext, ush, conv) = refs[pos:]

    qb = q_ref.shape[1]
    qi = pl.program_id(1)
    nq = pl.num_programs(1)

    lane = lax.broadcasted_iota(jnp.int32, (qb, LANES), 1)
    low_half = lane < HEAD_DIM
    row2 = lax.broadcasted_iota(jnp.int32, (2 * qb, 1), 0)
    if has_local:
        r = lax.broadcasted_iota(jnp.int32, (2 * qb, qb), 0) & (qb - 1)
        c = lax.broadcasted_iota(jnp.int32, (2 * qb, qb), 1)
        mask_prev = jnp.logical_and(c >= r, qi > 0)
        mask_next = jnp.logical_and(c <= r, qi < nq - 1)

    def qk(a, b):
        return lax.dot_general(a, b, (((1,), (1,)), ((), ())), preferred_element_type=F32)

    def with_ones(v):
        return jnp.concatenate([v, jnp.ones_like(v)], axis=1)

    for h in range(N_KV_HEADS):
        hs = slice(LANES * h, LANES * (h + 1))
        keys = [kc_ref[0, :, hs]]
        vals = [with_ones(vc_ref[0, :, hs])]
        if has_local:
            keys = [kp_ref[0, :, hs], ks_ref[0, :, hs], kn_ref[0, :, hs]] + keys
            vals = [with_ones(vp_ref[0, :, hs]), with_ones(vs_ref[0, :, hs]),
                    with_ones(vn_ref[0, :, hs])] + vals
        for pr in range(2):
            c0 = 2 * LANES * h + LANES * pr
            qp = q_ref[0, :, c0:c0 + LANES]
            zero = jnp.zeros_like(qp)
            qq = jnp.concatenate([jnp.where(low_half, qp, zero),
                                  jnp.where(low_half, zero, qp)], axis=0)
            sink_col = jnp.where(row2 < qb, sink_ref[4 * h + 2 * pr], sink_ref[4 * h + 2 * pr + 1])
            scores = [qk(qq, k) for k in keys]
            if has_local:
                scores[0] = jnp.where(mask_prev, scores[0], NEG_INF)
                scores[2] = jnp.where(mask_next, scores[2], NEG_INF)
            s = jnp.concatenate(scores, axis=1)
            m = jnp.maximum(jnp.max(s, axis=-1, keepdims=True), sink_col)
            p = jnp.exp(s - m).astype(BF16)
            ov = None
            col = 0
            for v in vals:
                n = v.shape[0]
                part = jnp.dot(p[:, col:col + n], v, preferred_element_type=F32)
                ov = part if ov is None else ov + part
                col += n
            denom = ov[:, LANES:] + jnp.exp(sink_col - m)
            o = ov[:, :LANES] * (1.0 / denom)
            att = jnp.where(low_half, o[:qb], o[qb:])
            gate = gz_ref[0, :, D_CONV + c0:D_CONV + c0 + LANES].astype(F32)
            act_ref[0, :, D_CONV + c0:D_CONV + c0 + LANES] = (att * gate).astype(BF16)

    ext = qb + 2 * HALO
    uext[0:HALO] = jnp.where(qi > 0, up_ref[0], 0.0)
    uext[HALO:HALO + qb] = u_ref[0]
    uext[HALO + qb:ext] = jnp.where(qi < nq - 1, un_ref[0], 0.0)

    for c in range(D_CONV // CONV_LANES):
        cs = slice(CONV_LANES * c, CONV_LANES * (c + 1))
        for r in range(SUBLANES):
            ush[c, r] = uext[r:r + ext - SUBLANES, cs]
        for rc in range(qb // CONV_ROWS):
            r0 = CONV_ROWS * rc
            acc = jnp.broadcast_to(dwb_ref[:, cs], (CONV_ROWS, CONV_LANES))
            for k in range(CONF_WIDTH):
                a, r = divmod(k + HALO - CONF_WIDTH // 2, SUBLANES)
                w = pltpu.repeat(dww_ref[k, :, cs], CONV_ROWS // SUBLANES, 0)
                acc = acc + w * ush[c, r, r0 + SUBLANES * a:r0 + SUBLANES * a + CONV_ROWS, :]
            conv[r0:r0 + CONV_ROWS, cs] = acc

    cv = conv[...]
    mu = jnp.mean(cv, axis=-1, keepdims=True)
    d = cv - mu
    var = jnp.mean(d * d, axis=-1, keepdims=True)
    y = d * lax.rsqrt(var + EPS) * lng_ref[...] + lnb_ref[...]
    a = y * _sigmoid(y)
    act_ref[0, :, 0:D_CONV] = (a * gz_ref[0, :, 0:D_CONV].astype(F32)).astype(BF16)


def _mixer(q, kd, vd, kcd, vcd, sink, u, gz, dw_w, dw_b, ln_g, ln_b, *, has_local):
    bsz, s_len, qw = q.shape
    n_ctx = kcd.shape[1]
    kv_w = kcd.shape[2]
    nq = s_len // Q_BLOCK
    nh = Q_BLOCK // HALO
    n_halo_blocks = s_len // HALO
    blk = lambda width: pl.BlockSpec((1, Q_BLOCK, width), lambda b, i: (b, i, 0))
    in_specs = [pl.BlockSpec(memory_space=pltpu.SMEM), blk(qw)]
    args = [sink, q]
    if has_local:
        prev = pl.BlockSpec((1, Q_BLOCK, kv_w), lambda b, i: (b, jnp.maximum(i - 1, 0), 0))
        nxt = pl.BlockSpec((1, Q_BLOCK, kv_w), lambda b, i: (b, jnp.minimum(i + 1, nq - 1), 0))
        in_specs += [prev, blk(kv_w), nxt, prev, blk(kv_w), nxt]
        args += [kd, kd, kd, vd, vd, vd]
    cspec = pl.BlockSpec((1, n_ctx, kv_w), lambda b, i: (b, 0, 0))
    in_specs += [
        cspec, cspec,
        blk(D_CONV),
        pl.BlockSpec((1, HALO, D_CONV), lambda b, i: (b, jnp.maximum(i * nh - 1, 0), 0)),
        pl.BlockSpec((1, HALO, D_CONV),
                     lambda b, i: (b, jnp.minimum((i + 1) * nh, n_halo_blocks - 1), 0)),
        blk(D_INNER),
        _const_spec((CONF_WIDTH, SUBLANES, D_CONV)),
        _const_spec((1, D_CONV)),
        _const_spec((1, D_CONV)),
        _const_spec((1, D_CONV)),
    ]
    dww = jnp.broadcast_to(dw_w[:, None, :], (CONF_WIDTH, SUBLANES, D_CONV))
    args += [kcd, vcd, u, u, u, gz, dww, dw_b, ln_g, ln_b]
    ext = Q_BLOCK + 2 * HALO
    return pl.pallas_call(
        functools.partial(_mixer_kernel, has_local=has_local),
        grid=(bsz, nq),
        in_specs=in_specs,
        out_specs=blk(D_INNER),
        out_shape=jax.ShapeDtypeStruct((bsz, s_len, D_INNER), BF16),
        scratch_shapes=[
            pltpu.VMEM((ext, D_CONV), F32),
            pltpu.VMEM((D_CONV // CONV_LANES, SUBLANES, ext - SUBLANES, CONV_LANES), F32),
            pltpu.VMEM((Q_BLOCK, D_CONV), F32),
        ],
        compiler_params=_params(2),
        name="mixer_local" if has_local else "mixer_ctx",
    )(*args)


def _ev_out_kernel(act_ref, x_ref, mod_ref, wout_ref, o_ref):
    yo = jnp.dot(act_ref[0], wout_ref[...], preferred_element_type=F32)
    o_ref[0] = x_ref[0] + mod_ref[0, 2:3, :] * yo


def _ev_out(act, x, mod, w_out, *, tm):
    bsz, s_len, _ = x.shape
    tok = lambda width: pl.BlockSpec((1, tm, width), lambda b, t: (b, t, 0))
    return pl.pallas_call(
        _ev_out_kernel,
        grid=(bsz, s_len // tm),
        in_specs=[tok(D_INNER), tok(D_MODEL),
                  pl.BlockSpec((1, 3, D_MODEL), lambda b, t: (b, 0, 0)),
                  _const_spec((D_INNER, D_MODEL))],
        out_specs=tok(D_MODEL),
        out_shape=jax.ShapeDtypeStruct(x.shape, F32),
        compiler_params=_params(2),
        name="ev_out",
    )(act, x, mod, w_out)


def _od_kernel(*refs, final):
    refs = list(refs)
    x_ref, xp_ref, xn_ref, mod_ref, g_ref, win_ref, cw_ref, cb_ref, wout_ref = refs[:9]
    if final:
        fg_ref, o_ref = refs[9:11]
    else:
        o_ref = refs[9]
    tm = x_ref.shape[1]
    ext = tm + 2 * OD_HALO
    t = pl.program_id(1)
    nt = pl.num_programs(1)

    xe = jnp.concatenate([xp_ref[0], x_ref[0], xn_ref[0]], axis=0)
    xn = _rms_mod(xe, g_ref[...], mod_ref[0, 0:1, :], mod_ref[0, 1:2, :]).astype(BF16)

    row = lax.broadcasted_iota(jnp.int32, (ext, 1), 0)
    inside = jnp.logical_and(jnp.logical_or(row >= OD_HALO, t > 0),
                             jnp.logical_or(row < OD_HALO + tm, t < nt - 1))

    acc = jnp.zeros((tm, D_MODEL), F32)
    for j in range(D_INNER // OD_CHUNK):
        def proj(sec):
            c0 = sec * D_INNER + OD_CHUNK * j
            return jnp.dot(xn, win_ref[:, c0:c0 + OD_CHUNK], preferred_element_type=F32)
        cs = slice(OD_CHUNK * j, OD_CHUNK * (j + 1))
        bg, cg, uu, z = proj(0), proj(1), proj(2), proj(3)
        p = jnp.where(inside, cg * uu, 0.0)
        p_prev = pltpu.roll(p, 1, 0)
        p_next = pltpu.roll(p, ext - 1, 0)
        cv = (cw_ref[0:1, cs] * p_prev + cw_ref[1:2, cs] * p + cw_ref[2:3, cs] * p_next
              + cb_ref[:, cs])
        ya = ((bg * cv) * (z * _sigmoid(z)))[OD_HALO:OD_HALO + tm].astype(BF16)
        acc = acc + jnp.dot(ya, wout_ref[cs, :], preferred_element_type=F32)

    xo = x_ref[0] + mod_ref[0, 2:3, :] * acc
    if final:
        xo = xo * lax.rsqrt(jnp.mean(xo * xo, axis=-1, keepdims=True) + EPS) * fg_ref[...]
    o_ref[0] = xo


def _od_layer(x, mod, g, w_in, conv_w, conv_b, w_out, final_g, *, tm):
    bsz, s_len, _ = x.shape
    nh = tm // OD_HALO
    n_halo_blocks = s_len // OD_HALO
    final = final_g is not None
    tok = pl.BlockSpec((1, tm, D_MODEL), lambda b, t: (b, t, 0))
    in_specs = [
        tok,
        pl.BlockSpec((1, OD_HALO, D_MODEL), lambda b, t: (b, jnp.maximum(t * nh - 1, 0), 0)),
        pl.BlockSpec((1, OD_HALO, D_MODEL),
                     lambda b, t: (b, jnp.minimum((t + 1) * nh, n_halo_blocks - 1), 0)),
        pl.BlockSpec((1, 3, D_MODEL), lambda b, t: (b, 0, 0)),
        _const_spec((1, D_MODEL)),
        _const_spec(w_in.shape),
        _const_spec(conv_w.shape),
        _const_spec((1, D_INNER)),
        _const_spec(w_out.shape),
    ]
    args = [x, x, x, mod, g, w_in, conv_w, conv_b, w_out]
    if final:
        in_specs.append(_const_spec((1, D_MODEL)))
        args.append(final_g)
    return pl.pallas_call(
        functools.partial(_od_kernel, final=final),
        grid=(bsz, s_len // tm),
        in_specs=in_specs,
        out_specs=tok,
        out_shape=jax.ShapeDtypeStruct(x.shape, F32),
        compiler_params=_params(2),
        name="od_layer_final" if final else "od_layer",
    )(*args)


def _rope_tables(s_len):
    pos = jnp.arange(s_len)
    row = (pos // GRID_W).astype(F32)
    col = (pos % GRID_W).astype(F32)
    quarter = HEAD_DIM // 4
    freqs = ROPE_THETA ** (-jnp.arange(quarter, dtype=F32) / quarter)
    d = jnp.arange(LANES) % HEAD_DIM
    p = jnp.where((d >= HEAD_DIM // 2)[None, :], col[:, None], row[:, None])
    ang = p * freqs[d % quarter][None, :]
    sign = jnp.where((d % (HEAD_DIM // 2)) < quarter, -1.0, 1.0).astype(F32)
    return jnp.cos(ang), jnp.sin(ang) * sign[None, :]


def kernel(x, c, ctx, c_ctx, norm_g, ada_w, ada_b, ev_w_in, ev_dw_w, ev_dw_b, ev_ln_g, ev_ln_b,
           ev_sink, ev_w_out, od_w_in, od_conv_w, od_conv_b, od_w_out, final_g):
    bsz, s_len, _ = x.shape
    n_ctx = ctx.shape[1]
    assert s_len % 512 == 0 and n_ctx % Q_BLOCK == 0 and bsz + 1 <= 16

    cond = jnp.zeros((16, D_MODEL), F32).at[:bsz].set(c).at[bsz].set(c_ctx)
    mods = _adaln_all(cond, ada_w, ada_b)
    cos_sin = _rope_tables(s_len)

    for i in range(DEPTH):
        mod_x = mods[i, :bsz].reshape(bsz, 3, D_MODEL)
        mod_c = jnp.broadcast_to(mods[i, bsz].reshape(1, 3, D_MODEL), (bsz, 3, D_MODEL))
        g = norm_g[i].reshape(1, D_MODEL)
        ctx_out_needed = any(j % 2 == 0 for j in range(i + 1, DEPTH))
        if i % 2 == 0:
            e = i // 2
            w_in = ev_w_in[e].astype(BF16)
            w_out = ev_w_out[e].astype(BF16)
            conv_args = (ev_dw_w[e], ev_dw_b[e].reshape(1, D_CONV), ev_ln_g[e].reshape(1, D_CONV),
                         ev_ln_b[e].reshape(1, D_CONV))
            if ctx_out_needed:
                uc, qc, kcd, vcd, gzc = _ev_in(ctx, mod_c, g, w_in, None, kv_only=False, tm=n_ctx)
            else:
                kcd, vcd = _ev_in(ctx, mod_c, g, w_in[:, EV_K:EV_G], None, kv_only=True, tm=n_ctx)
            u, q, kd, vd, gz = _ev_in(x, mod_x, g, w_in, cos_sin, kv_only=False, tm=512)
            act = _mixer(q, kd, vd, kcd, vcd, ev_sink[e], u, gz, *conv_args, has_local=True)
            x = _ev_out(act, x, mod_x, w_out, tm=512)
            if ctx_out_needed:
                act_c = _mixer(qc, None, None, kcd, vcd, ev_sink[e], uc, gzc, *conv_args,
                               has_local=False)
                ctx = _ev_out(act_c, ctx, mod_c, w_out, tm=n_ctx)
        else:
            o = i // 2
            od_args = (od_w_in[o].astype(BF16), od_conv_w[o], od_conv_b[o].reshape(1, D_INNER),
                       od_w_out[o].astype(BF16))
            fg = final_g.reshape(1, D_MODEL) if i == DEPTH - 1 else None
            x = _od_layer(x, mod_x, g, *od_args, fg, tm=512)
            if ctx_out_needed:
                ctx = _od_layer(ctx, mod_c, g, *od_args, None, tm=n_ctx)
    return x
```

```python
import functools
import math

import jax
import jax.numpy as jnp
from jax import lax
from jax.experimental import pallas as pl
from jax.experimental.pallas import tpu as pltpu

F32 = jnp.float32
BF16 = jnp.bfloat16

D_MODEL = 1024
DEPTH = 4
GRID_W = 64
EPS = 1e-6
D_INNER = 2048
D_CONV = 1024
HEAD_DIM = 64
N_Q_HEADS = 16
N_KV_HEADS = 4
CONF_WIDTH = 31
WINDOW = 128
ROPE_THETA = 10000.0
NEG_INF = -1e30
EV_Q = 2 * D_CONV
EV_K = EV_Q + N_Q_HEADS * HEAD_DIM
EV_V = EV_K + N_KV_HEADS * HEAD_DIM
EV_G = EV_V + N_KV_HEADS * HEAD_DIM
EV_COLS = EV_G + D_INNER
KV_DUP = 2 * N_KV_HEADS * HEAD_DIM
LOG2E = math.log2(math.e)
Q_SCALE = HEAD_DIM ** -0.5 * LOG2E

LANES = 128
SUBLANES = 8
VMEM_LIMIT_BYTES = 58 * 1024 * 1024

Q_BLOCK = 128
EV_ROWS = 2 * Q_BLOCK
HALO = 16
CONV_ROWS = 32
CONV_LANES = 256
OD_HALO = 8
OD_CHUNK = 512


def _sigmoid(v):
    return 1.0 / (1.0 + jnp.exp(-v))


def _rms_mod(x, g, shift, scale):
    y = x * lax.rsqrt(jnp.mean(x * x, axis=-1, keepdims=True) + EPS) * g
    return y * (1.0 + scale) + shift


def _const_spec(shape):
    nd = len(shape)
    return pl.BlockSpec(shape, lambda *_: (0,) * nd, pipeline_mode=pl.Buffered(1))


def _params(n_axes):
    return pltpu.CompilerParams(
        dimension_semantics=("arbitrary",) * n_axes,
        vmem_limit_bytes=VMEM_LIMIT_BYTES)


def _adaln_kernel(cond_ref, w_ref, b_ref, o_ref):
    cnd = cond_ref[...]
    a = (cnd * _sigmoid(cnd)).astype(BF16)
    o_ref[0] = jnp.dot(a, w_ref[0].astype(BF16), preferred_element_type=F32) + b_ref[0]


def _adaln_all(cond, ada_w, ada_b):
    rows = cond.shape[0]
    tn = 1024
    return pl.pallas_call(
        _adaln_kernel,
        grid=(DEPTH, 3 * D_MODEL // tn),
        in_specs=[
            pl.BlockSpec((rows, D_MODEL), lambda i, j: (0, 0)),
            pl.BlockSpec((1, D_MODEL, tn), lambda i, j: (i, 0, j)),
            pl.BlockSpec((1, 1, tn), lambda i, j: (i, 0, j)),
        ],
        out_specs=pl.BlockSpec((1, rows, tn), lambda i, j: (i, 0, j)),
        out_shape=jax.ShapeDtypeStruct((DEPTH, rows, 3 * D_MODEL), F32),
        compiler_params=_params(2),
        name="adaln",
    )(cond, ada_w, ada_b.reshape(DEPTH, 1, 3 * D_MODEL))


def _rope_slab(t, cos, sin, first_half):
    partner = jnp.where(first_half, pltpu.roll(t, LANES - 16, 1), pltpu.roll(t, 16, 1))
    return t * cos + partner * sin


def _dup_heads(s, low_half):
    r = pltpu.roll(s, HEAD_DIM, 1)
    return jnp.where(low_half, s, r), jnp.where(low_half, r, s)


def _in_proj_steps(xn, w_ref, cos_sin, u_out, q_out, kd_out, vd_out, gz_out, *, kv_only):
    rows = xn.shape[0]
    k_col = 0 if kv_only else EV_K
    v_col = k_col + N_KV_HEADS * HEAD_DIM

    def proj(c0, width):
        return jnp.dot(xn, w_ref[:, c0:c0 + width], preferred_element_type=F32)

    lane = lax.broadcasted_iota(jnp.int32, (rows, LANES), 1)
    low_half = lane < HEAD_DIM
    first_half = (lane & 31) < 16
    rope = cos_sin is not None
    if rope:
        cos, sin = cos_sin

    def kv_step():
        kk = proj(k_col, N_KV_HEADS * HEAD_DIM)
        vv = proj(v_col, N_KV_HEADS * HEAD_DIM)
        for s in range(N_KV_HEADS * HEAD_DIM // LANES):
            ks = kk[:, LANES * s:LANES * (s + 1)]
            if rope:
                ks = _rope_slab(ks, cos, sin, first_half)
            k0, k1 = _dup_heads(ks, low_half)
            kd_out[:, 2 * LANES * s:2 * LANES * s + LANES] = k0.astype(BF16)
            kd_out[:, 2 * LANES * s + LANES:2 * LANES * (s + 1)] = k1.astype(BF16)
            v0, v1 = _dup_heads(vv[:, LANES * s:LANES * (s + 1)], low_half)
            vd_out[:, 2 * LANES * s:2 * LANES * s + LANES] = v0.astype(BF16)
            vd_out[:, 2 * LANES * s + LANES:2 * LANES * (s + 1)] = v1.astype(BF16)

    def glu_step(j):
        val = proj(512 * j, 512)
        gate = proj(D_CONV + 512 * j, 512)
        u_out[:, 512 * j:512 * (j + 1)] = val * _sigmoid(gate)

    def q_step(j):
        t = proj(EV_Q + 512 * j, 512)
        for s in range(512 // LANES):
            slab = t[:, LANES * s:LANES * (s + 1)]
            if rope:
                slab = _rope_slab(slab, cos, sin, first_half)
            c0 = 512 * j + LANES * s
            q_out[:, c0:c0 + LANES] = (slab * Q_SCALE).astype(BF16)

    def gz_step(j):
        z = proj(EV_G + 512 * j, 512)
        gz_out[:, 512 * j:512 * (j + 1)] = (z * _sigmoid(z)).astype(BF16)

    steps = [kv_step]
    if not kv_only:
        steps += [functools.partial(glu_step, j) for j in range(D_CONV // 512)]
        steps += [functools.partial(q_step, j) for j in range(N_Q_HEADS * HEAD_DIM // 512)]
        steps += [functools.partial(gz_step, j) for j in range(D_INNER // 512)]
    return steps


def _attn_steps(sink_ref, q, keys, vals, gz, has_prev, has_next, act_out, *, has_local):
    qb = q.shape[0]
    lane = lax.broadcasted_iota(jnp.int32, (qb, LANES), 1)
    low_half = lane < HEAD_DIM
    row2 = lax.broadcasted_iota(jnp.int32, (2 * qb, 1), 0)
    if has_local:
        r = lax.broadcasted_iota(jnp.int32, (2 * qb, qb), 0) & (qb - 1)
        c = lax.broadcasted_iota(jnp.int32, (2 * qb, qb), 1)
        mask_prev = jnp.logical_and(c >= r, has_prev)
        mask_next = jnp.logical_and(c <= r, has_next)

    def qk(a, b):
        return lax.dot_general(a, b, (((1,), (1,)), ((), ())), preferred_element_type=F32)

    def with_ones(v):
        return jnp.concatenate([v, jnp.ones_like(v)], axis=1)

    def head_step(h):
        hs = slice(LANES * h, LANES * (h + 1))
        kh = [k[:, hs] for k in keys]
        vh = [with_ones(v[:, hs]) for v in vals]
        for pr in range(2):
            c0 = 2 * LANES * h + LANES * pr
            qp = q[:, c0:c0 + LANES]
            zero = jnp.zeros_like(qp)
            qq = jnp.concatenate([jnp.where(low_half, qp, zero),
                                  jnp.where(low_half, zero, qp)], axis=0)
            sink_col = LOG2E * jnp.where(row2 < qb, sink_ref[4 * h + 2 * pr],
                                         sink_ref[4 * h + 2 * pr + 1])
            scores = [qk(qq, k) for k in kh]
            if has_local:
                scores[0] = jnp.where(mask_prev, scores[0], NEG_INF)
                scores[2] = jnp.where(mask_next, scores[2], NEG_INF)
            s = jnp.concatenate(scores, axis=1)
            m = jnp.maximum(jnp.max(s, axis=-1, keepdims=True), sink_col)
            p = jnp.exp2(s - m).astype(BF16)
            ov = None
            col = 0
            for v in vh:
                n = v.shape[0]
                part = jnp.dot(p[:, col:col + n], v, preferred_element_type=F32)
                ov = part if ov is None else ov + part
                col += n
            denom = ov[:, LANES:] + jnp.exp2(sink_col - m)
            o = ov[:, :LANES] * (1.0 / denom)
            att = jnp.where(low_half, o[:qb], o[qb:])
            gate = gz[:, D_CONV + c0:D_CONV + c0 + LANES].astype(F32)
            act_out[:, D_CONV + c0:D_CONV + c0 + LANES] = (att * gate).astype(BF16)

    return [functools.partial(head_step, h) for h in range(N_KV_HEADS)]


def _conv_steps(u_prev, u_self, u_next, gz, has_prev, has_next, dww_ref, dwb_ref, lng_ref, lnb_ref,
                act_out, uext, ush, conv):
    qb = u_self.shape[0]
    ext = qb + 2 * HALO

    def chunk_step(c):
        if c == 0:
            uext[0:HALO] = jnp.where(has_prev, u_prev[...], 0.0)
            uext[HALO:HALO + qb] = u_self[...]
            uext[HALO + qb:ext] = jnp.where(has_next, u_next[...], 0.0)
        cs = slice(CONV_LANES * c, CONV_LANES * (c + 1))
        for r in range(SUBLANES):
            ush[c, r] = uext[r:r + ext - SUBLANES, cs]
        for rc in range(qb // CONV_ROWS):
            r0 = CONV_ROWS * rc
            acc = jnp.broadcast_to(dwb_ref[:, cs], (CONV_ROWS, CONV_LANES))
            for k in range(CONF_WIDTH):
                a, r = divmod(k + HALO - CONF_WIDTH // 2, SUBLANES)
                w = jnp.tile(dww_ref[k, :, cs], (CONV_ROWS // SUBLANES, 1))
                acc = acc + w * ush[c, r, r0 + SUBLANES * a:r0 + SUBLANES * a + CONV_ROWS, :]
            conv[r0:r0 + CONV_ROWS, cs] = acc

    def norm_step():
        cv = conv[...]
        mu = jnp.mean(cv, axis=-1, keepdims=True)
        d = cv - mu
        var = jnp.mean(d * d, axis=-1, keepdims=True)
        y = d * lax.rsqrt(var + EPS) * lng_ref[...] + lnb_ref[...]
        a = y * _sigmoid(y)
        act_out[:, 0:D_CONV] = (a * gz[:, 0:D_CONV].astype(F32)).astype(BF16)

    return [functools.partial(chunk_step, c) for c in range(D_CONV // CONV_LANES)] + [norm_step]


def _conv_scratch(n):
    ext = Q_BLOCK + 2 * HALO
    return [
        pltpu.VMEM((n, ext, D_CONV), F32),
        pltpu.VMEM((n, D_CONV // CONV_LANES, SUBLANES, ext - SUBLANES, CONV_LANES), F32),
        pltpu.VMEM((n, Q_BLOCK, D_CONV), F32),
    ]


def _bcast_taps(dw_w):
    return jnp.broadcast_to(dw_w[:, None, :], (CONF_WIDTH, SUBLANES, D_CONV))


def _ev_layer_kernel(sink_ref, xa_ref, xb_ref, moda_ref, modb_ref, g_ref, win_ref, cos_ref, sin_ref,
                     kc_ref, vc_ref, dww_ref, dwb_ref, lng_ref, lnb_ref, wout_ref,
                     o_ref,
                     new_u, new_q, new_kd, new_vd, new_gz,
                     cur_u, cur_q, cur_kd, cur_vd, cur_gz,
                     tail_u, tail_kd, tail_vd,
                     uext, ush, conv, act, *, blocks_per_seq):
    g = pl.program_id(0)

    @pl.when(g == 0)
    def _():
        for ref in (cur_u, cur_q, cur_kd, cur_vd, cur_gz, tail_u, tail_kd, tail_vd):
            ref[...] = jnp.zeros(ref.shape, ref.dtype)

    xn = _rms_mod(xa_ref[...], g_ref[...], moda_ref[0, 0:1, :], moda_ref[0, 1:2, :]).astype(BF16)
    proj = _in_proj_steps(xn, win_ref, (cos_ref[...], sin_ref[...]), new_u, new_q, new_kd, new_vd,
                          new_gz, kv_only=False)

    pos = lax.rem(g + blocks_per_seq - 1, blocks_per_seq)
    n_sub = EV_ROWS // Q_BLOCK
    attn, cnv = [], []
    for j in range(n_sub):
        rows = pl.ds(Q_BLOCK * j, Q_BLOCK)
        if j == 0:
            k_prev, v_prev, u_prev = tail_kd, tail_vd, tail_u
        else:
            before = pl.ds(Q_BLOCK * (j - 1), Q_BLOCK)
            k_prev, v_prev = cur_kd.at[before], cur_vd.at[before]
            u_prev = cur_u.at[pl.ds(Q_BLOCK * j - HALO, HALO)]
        if j == n_sub - 1:
            head = pl.ds(0, Q_BLOCK)
            k_next, v_next, u_next = new_kd.at[head], new_vd.at[head], new_u.at[pl.ds(0, HALO)]
        else:
            after = pl.ds(Q_BLOCK * (j + 1), Q_BLOCK)
            k_next, v_next = cur_kd.at[after], cur_vd.at[after]
            u_next = cur_u.at[pl.ds(Q_BLOCK * (j + 1), HALO)]
        sub = n_sub * pos + j
        has_prev = sub > 0
        has_next = sub < n_sub * blocks_per_seq - 1
        attn.append(_attn_steps(
            sink_ref, cur_q.at[rows],
            [k_prev, cur_kd.at[rows], k_next, kc_ref.at[0]],
            [v_prev, cur_vd.at[rows], v_next, vc_ref.at[0]],
            cur_gz.at[rows], has_prev, has_next, act.at[rows], has_local=True))
        cnv.append(_conv_steps(
            u_prev, cur_u.at[rows], u_next, cur_gz.at[rows], has_prev, has_next,
            dww_ref, dwb_ref, lng_ref, lnb_ref, act.at[rows], uext.at[j], ush.at[j], conv.at[j]))

    proj[0]()
    nxt = 1
    for j in range(n_sub):
        for c in range(N_KV_HEADS):
            cnv[j][c]()
            proj[nxt]()
            nxt += 1
            attn[j][c]()
        cnv[j][N_KV_HEADS]()
    assert nxt == len(proj) and len(cnv[0]) == N_KV_HEADS + 1

    yo = jnp.dot(act[...], wout_ref[...], preferred_element_type=F32)
    o_ref[...] = xb_ref[...] + modb_ref[0, 2:3, :] * yo

    last = pl.ds(EV_ROWS - Q_BLOCK, Q_BLOCK)
    tail_kd[...] = cur_kd[last, :]
    tail_vd[...] = cur_vd[last, :]
    tail_u[...] = cur_u[pl.ds(EV_ROWS - HALO, HALO), :]
    cur_u[...] = new_u[...]
    cur_q[...] = new_q[...]
    cur_kd[...] = new_kd[...]
    cur_vd[...] = new_vd[...]
    cur_gz[...] = new_gz[...]


def _ev_layer(x, mod, g, w_in, cos_sin, kcd, vcd, sink, dw_w, dw_b, ln_g, ln_b, w_out):
    bsz, s_len, _ = x.shape
    n_ctx = kcd.shape[1]
    bps = s_len // EV_ROWS
    n_blocks = bsz * bps
    x2 = x.reshape(bsz * s_len, D_MODEL)
    blk_a = lambda t: jnp.minimum(t, n_blocks - 1)
    blk_b = lambda t: jnp.maximum(t - 1, 0)
    in_specs = [
        pl.BlockSpec(memory_space=pltpu.SMEM),
        pl.BlockSpec((EV_ROWS, D_MODEL), lambda t: (blk_a(t), 0)),
        pl.BlockSpec((EV_ROWS, D_MODEL), lambda t: (blk_b(t), 0)),
        pl.BlockSpec((1, 3, D_MODEL), lambda t: (blk_a(t) // bps, 0, 0)),
        pl.BlockSpec((1, 3, D_MODEL), lambda t: (blk_b(t) // bps, 0, 0)),
        _const_spec((1, D_MODEL)),
        _const_spec(w_in.shape),
        pl.BlockSpec((EV_ROWS, LANES), lambda t: (blk_a(t) % bps, 0)),
        pl.BlockSpec((EV_ROWS, LANES), lambda t: (blk_a(t) % bps, 0)),
        pl.BlockSpec((1, n_ctx, KV_DUP), lambda t: (blk_b(t) // bps, 0, 0)),
        pl.BlockSpec((1, n_ctx, KV_DUP), lambda t: (blk_b(t) // bps, 0, 0)),
        _const_spec((CONF_WIDTH, SUBLANES, D_CONV)),
        _const_spec((1, D_CONV)),
        _const_spec((1, D_CONV)),
        _const_spec((1, D_CONV)),
        _const_spec((D_INNER, D_MODEL)),
    ]
    state = lambda rows: [
        pltpu.VMEM((rows, D_CONV), F32),
        pltpu.VMEM((rows, N_Q_HEADS * HEAD_DIM), BF16),
        pltpu.VMEM((rows, KV_DUP), BF16),
        pltpu.VMEM((rows, KV_DUP), BF16),
        pltpu.VMEM((rows, D_INNER), BF16),
    ]
    scratch = (state(EV_ROWS) + state(EV_ROWS)
               + [pltpu.VMEM((HALO, D_CONV), F32), pltpu.VMEM((Q_BLOCK, KV_DUP), BF16),
                  pltpu.VMEM((Q_BLOCK, KV_DUP), BF16)]
               + _conv_scratch(EV_ROWS // Q_BLOCK)
               + [pltpu.VMEM((EV_ROWS, D_INNER), BF16)])
    out = pl.pallas_call(
        functools.partial(_ev_layer_kernel, blocks_per_seq=bps),
        grid=(n_blocks + 1,),
        in_specs=in_specs,
        out_specs=pl.BlockSpec((EV_ROWS, D_MODEL), lambda t: (blk_b(t), 0)),
        out_shape=jax.ShapeDtypeStruct(x2.shape, F32),
        scratch_shapes=scratch,
        compiler_params=_params(1),
        name="ev_layer",
    )(sink, x2, x2, mod, mod, g, w_in, *cos_sin, kcd, vcd, _bcast_taps(dw_w), dw_b, ln_g, ln_b,
      w_out)
    return out.reshape(x.shape)


def _ev_in_kernel(*refs, kv_only):
    x_ref, mod_ref, g_ref, w_ref = refs[:4]
    xn = _rms_mod(x_ref[0], g_ref[...], mod_ref[0, 0:1, :], mod_ref[0, 1:2, :]).astype(BF16)
    if kv_only:
        kd_ref, vd_ref = refs[4:]
        steps = _in_proj_steps(xn, w_ref, None, None, None, kd_ref.at[0], vd_ref.at[0], None,
                               kv_only=True)
    else:
        u_ref, q_ref, kd_ref, vd_ref, gz_ref = refs[4:]
        steps = _in_proj_steps(xn, w_ref, None, u_ref.at[0], q_ref.at[0], kd_ref.at[0],
                               vd_ref.at[0], gz_ref.at[0], kv_only=False)
    for step in steps:
        step()


def _ev_in_ctx(x, mod, g, w, *, kv_only):
    bsz, s_len, _ = x.shape
    tok = lambda width: pl.BlockSpec((1, s_len, width), lambda b: (b, 0, 0))
    shp = lambda width, dt: jax.ShapeDtypeStruct((bsz, s_len, width), dt)
    if kv_only:
        out_specs = [tok(KV_DUP), tok(KV_DUP)]
        out_shape = [shp(KV_DUP, BF16), shp(KV_DUP, BF16)]
    else:
        out_specs = [tok(D_CONV), tok(N_Q_HEADS * HEAD_DIM), tok(KV_DUP), tok(KV_DUP), tok(D_INNER)]
        out_shape = [shp(D_CONV, F32), shp(N_Q_HEADS * HEAD_DIM, BF16), shp(KV_DUP, BF16),
                     shp(KV_DUP, BF16), shp(D_INNER, BF16)]
    return pl.pallas_call(
        functools.partial(_ev_in_kernel, kv_only=kv_only),
        grid=(bsz,),
        in_specs=[tok(D_MODEL), pl.BlockSpec((1, 3, D_MODEL), lambda b: (b, 0, 0)),
                  _const_spec((1, D_MODEL)), _const_spec(w.shape)],
        out_specs=out_specs,
        out_shape=out_shape,
        compiler_params=_params(1),
        name="ev_in_ctx_kv" if kv_only else "ev_in_ctx",
    )(x, mod, g, w)


def _mixer_ctx_kernel(sink_ref, q_ref, kc_ref, vc_ref, u_ref, up_ref, un_ref, gz_ref, dww_ref,
                      dwb_ref, lng_ref, lnb_ref, act_ref, uext, ush, conv):
    qi = pl.program_id(1)
    has_prev = qi > 0
    has_next = qi < pl.num_programs(1) - 1
    steps = _attn_steps(sink_ref, q_ref.at[0], [kc_ref.at[0]], [vc_ref.at[0]], gz_ref.at[0],
                        has_prev, has_next, act_ref.at[0], has_local=False)
    steps += _conv_steps(up_ref.at[0], u_ref.at[0], un_ref.at[0], gz_ref.at[0], has_prev, has_next,
                         dww_ref, dwb_ref, lng_ref, lnb_ref, act_ref.at[0],
                         uext.at[0], ush.at[0], conv.at[0])
    for step in steps:
        step()


def _mixer_ctx(q, kcd, vcd, sink, u, gz, dw_w, dw_b, ln_g, ln_b):
    bsz, s_len, qw = q.shape
    nq = s_len // Q_BLOCK
    nh = Q_BLOCK // HALO
    n_halo_blocks = s_len // HALO
    blk = lambda width: pl.BlockSpec((1, Q_BLOCK, width), lambda b, i: (b, i, 0))
    cspec = pl.BlockSpec((1, s_len, KV_DUP), lambda b, i: (b, 0, 0))
    in_specs = [
        pl.BlockSpec(memory_space=pltpu.SMEM), blk(qw), cspec, cspec, blk(D_CONV),
        pl.BlockSpec((1, HALO, D_CONV), lambda b, i: (b, jnp.maximum(i * nh - 1, 0), 0)),
        pl.BlockSpec((1, HALO, D_CONV),
                     lambda b, i: (b, jnp.minimum((i + 1) * nh, n_halo_blocks - 1), 0)),
        blk(D_INNER),
        _const_spec((CONF_WIDTH, SUBLANES, D_CONV)),
        _const_spec((1, D_CONV)), _const_spec((1, D_CONV)), _const_spec((1, D_CONV)),
    ]
    return pl.pallas_call(
        _mixer_ctx_kernel,
        grid=(bsz, nq),
        in_specs=in_specs,
        out_specs=blk(D_INNER),
        out_shape=jax.ShapeDtypeStruct((bsz, s_len, D_INNER), BF16),
        scratch_shapes=_conv_scratch(1),
        compiler_params=_params(2),
        name="mixer_ctx",
    )(sink, q, kcd, vcd, u, u, u, gz, _bcast_taps(dw_w), dw_b, ln_g, ln_b)


def _ev_out_kernel(act_ref, x_ref, mod_ref, wout_ref, o_ref):
    yo = jnp.dot(act_ref[0], wout_ref[...], preferred_element_type=F32)
    o_ref[0] = x_ref[0] + mod_ref[0, 2:3, :] * yo


def _ev_out_ctx(act, x, mod, w_out):
    bsz, s_len, _ = x.shape
    tok = lambda width: pl.BlockSpec((1, s_len, width), lambda b: (b, 0, 0))
    return pl.pallas_call(
        _ev_out_kernel,
        grid=(bsz,),
        in_specs=[tok(D_INNER), tok(D_MODEL), pl.BlockSpec((1, 3, D_MODEL), lambda b: (b, 0, 0)),
                  _const_spec((D_INNER, D_MODEL))],
        out_specs=tok(D_MODEL),
        out_shape=jax.ShapeDtypeStruct(x.shape, F32),
        compiler_params=_params(1),
        name="ev_out_ctx",
    )(act, x, mod, w_out)


def _od_kernel(*refs, final):
    refs = list(refs)
    x_ref, xp_ref, xn_ref, mod_ref, g_ref, win_ref, cw_ref, cb_ref, wout_ref = refs[:9]
    if final:
        fg_ref, o_ref = refs[9:11]
    else:
        o_ref = refs[9]
    tm = x_ref.shape[1]
    ext = tm + 2 * OD_HALO
    t = pl.program_id(1)
    nt = pl.num_programs(1)

    xe = jnp.concatenate([xp_ref[0], x_ref[0], xn_ref[0]], axis=0)
    xn = _rms_mod(xe, g_ref[...], mod_ref[0, 0:1, :], mod_ref[0, 1:2, :]).astype(BF16)

    row = lax.broadcasted_iota(jnp.int32, (ext, 1), 0)
    inside = jnp.logical_and(jnp.logical_or(row >= OD_HALO, t > 0),
                             jnp.logical_or(row < OD_HALO + tm, t < nt - 1))

    acc = jnp.zeros((tm, D_MODEL), F32)
    for j in range(D_INNER // OD_CHUNK):
        def proj(sec):
            c0 = sec * D_INNER + OD_CHUNK * j
            return jnp.dot(xn, win_ref[:, c0:c0 + OD_CHUNK], preferred_element_type=F32)
        cs = slice(OD_CHUNK * j, OD_CHUNK * (j + 1))
        bg, cg, uu, z = proj(0), proj(1), proj(2), proj(3)
        p = jnp.where(inside, cg * uu, 0.0)
        p_prev = pltpu.roll(p, 1, 0)
        p_next = pltpu.roll(p, ext - 1, 0)
        cv = (cw_ref[0:1, cs] * p_prev + cw_ref[1:2, cs] * p + cw_ref[2:3, cs] * p_next
              + cb_ref[:, cs])
        ya = ((bg * cv) * (z * _sigmoid(z)))[OD_HALO:OD_HALO + tm].astype(BF16)
        acc = acc + jnp.dot(ya, wout_ref[cs, :], preferred_element_type=F32)

    xo = x_ref[0] + mod_ref[0, 2:3, :] * acc
    if final:
        xo = xo * lax.rsqrt(jnp.mean(xo * xo, axis=-1, keepdims=True) + EPS) * fg_ref[...]
    o_ref[0] = xo


def _od_layer(x, mod, g, w_in, conv_w, conv_b, w_out, final_g, *, tm):
    bsz, s_len, _ = x.shape
    nh = tm // OD_HALO
    n_halo_blocks = s_len // OD_HALO
    final = final_g is not None
    tok = pl.BlockSpec((1, tm, D_MODEL), lambda b, t: (b, t, 0))
    in_specs = [
        tok,
        pl.BlockSpec((1, OD_HALO, D_MODEL), lambda b, t: (b, jnp.maximum(t * nh - 1, 0), 0)),
        pl.BlockSpec((1, OD_HALO, D_MODEL),
                     lambda b, t: (b, jnp.minimum((t + 1) * nh, n_halo_blocks - 1), 0)),
        pl.BlockSpec((1, 3, D_MODEL), lambda b, t: (b, 0, 0)),
        _const_spec((1, D_MODEL)),
        _const_spec(w_in.shape),
        _const_spec(conv_w.shape),
        _const_spec((1, D_INNER)),
        _const_spec(w_out.shape),
    ]
    args = [x, x, x, mod, g, w_in, conv_w, conv_b, w_out]
    if final:
        in_specs.append(_const_spec((1, D_MODEL)))
        args.append(final_g)
    return pl.pallas_call(
        functools.partial(_od_kernel, final=final),
        grid=(bsz, s_len // tm),
        in_specs=in_specs,
        out_specs=tok,
        out_shape=jax.ShapeDtypeStruct(x.shape, F32),
        compiler_params=_params(2),
        name="od_layer_final" if final else "od_layer",
    )(*args)


def _rope_tables(s_len):
    pos = jnp.arange(s_len)
    row = (pos // GRID_W).astype(F32)
    col = (pos % GRID_W).astype(F32)
    quarter = HEAD_DIM // 4
    freqs = ROPE_THETA ** (-jnp.arange(quarter, dtype=F32) / quarter)
    d = jnp.arange(LANES) % HEAD_DIM
    p = jnp.where((d >= HEAD_DIM // 2)[None, :], col[:, None], row[:, None])
    ang = p * freqs[d % quarter][None, :]
    sign = jnp.where((d % (HEAD_DIM // 2)) < quarter, -1.0, 1.0).astype(F32)
    return jnp.cos(ang), jnp.sin(ang) * sign[None, :]


def kernel(x, c, ctx, c_ctx, norm_g, ada_w, ada_b, ev_w_in, ev_dw_w, ev_dw_b, ev_ln_g, ev_ln_b,
           ev_sink, ev_w_out, od_w_in, od_conv_w, od_conv_b, od_w_out, final_g):
    bsz, s_len, _ = x.shape
    n_ctx = ctx.shape[1]
    assert s_len % 512 == 0 and n_ctx % Q_BLOCK == 0 and bsz + 1 <= 16

    cond = jnp.zeros((16, D_MODEL), F32).at[:bsz].set(c).at[bsz].set(c_ctx)
    mods = _adaln_all(cond, ada_w, ada_b)
    cos_sin = _rope_tables(s_len)

    for i in range(DEPTH):
        mod_x = mods[i, :bsz].reshape(bsz, 3, D_MODEL)
        mod_c = jnp.broadcast_to(mods[i, bsz].reshape(1, 3, D_MODEL), (bsz, 3, D_MODEL))
        g = norm_g[i].reshape(1, D_MODEL)
        ctx_out_needed = any(j % 2 == 0 for j in range(i + 1, DEPTH))
        if i % 2 == 0:
            e = i // 2
            w_in = ev_w_in[e].astype(BF16)
            w_out = ev_w_out[e].astype(BF16)
            conv_args = (ev_dw_w[e], ev_dw_b[e].reshape(1, D_CONV), ev_ln_g[e].reshape(1, D_CONV),
                         ev_ln_b[e].reshape(1, D_CONV))
            if ctx_out_needed:
                uc, qc, kcd, vcd, gzc = _ev_in_ctx(ctx, mod_c, g, w_in, kv_only=False)
            else:
                kcd, vcd = _ev_in_ctx(ctx, mod_c, g, w_in[:, EV_K:EV_G], kv_only=True)
            x = _ev_layer(x, mod_x, g, w_in, cos_sin, kcd, vcd, ev_sink[e], *conv_args, w_out)
            if ctx_out_needed:
                act_c = _mixer_ctx(qc, kcd, vcd, ev_sink[e], uc, gzc, *conv_args)
                ctx = _ev_out_ctx(act_c, ctx, mod_c, w_out)
        else:
            o = i // 2
            od_args = (od_w_in[o].astype(BF16), od_conv_w[o], od_conv_b[o].reshape(1, D_INNER),
                       od_w_out[o].astype(BF16))
            fg = final_g.reshape(1, D_MODEL) if i == DEPTH - 1 else None
            x = _od_layer(x, mod_x, g, *od_args, fg, tm=512)
            if ctx_out_needed:
                ctx = _od_layer(ctx, mod_c, g, *od_args, None, tm=n_ctx)
    return x
```
